```python
import math
import jax, jax.numpy as jnp
from jax import lax
import numpy as np

D_MODEL = 1024
BATCH = 1
SEQ = 16384
DEPTH = 1

D_MIX = D_MODEL
HEAD_DIM = 64
RET_WIDTH = D_MIX // 2
SB_WIDTH = D_MIX - RET_WIDTH
N_RET_HEADS = RET_WIDTH // HEAD_DIM
N_SB_HEADS = SB_WIDTH // HEAD_DIM
RET_CHUNK = 128
SB_BLOCK = 128
ROPE_BASE = 10000.0
IN_COLS = 4 * RET_WIDTH + 3 * SB_WIDTH
N_KEYS = 128
N_EXPERTS = N_KEYS * N_KEYS
PEER_HEADS = 8
PEER_TOPK = 16
PEER_QDIM = 256
PEER_HALF = PEER_QDIM // 2
PEER_BLOCK = 128
EPS = 1e-6

kernel_name = "hymba_retention_stickbreak_peer_adaln"


def rms_norm(x, gain):
    xf = x.astype(jnp.float32)
    y = xf * lax.rsqrt(jnp.mean(xf * xf, axis=-1, keepdims=True) + EPS)
    return (y * gain.astype(jnp.float32)).astype(x.dtype)


def rotary(x, positions):
    half = HEAD_DIM // 2
    inv_freq = ROPE_BASE ** (-jnp.arange(half, dtype=jnp.float32) / half)
    ang = positions.astype(jnp.float32)[..., None] * inv_freq
    cos = jnp.cos(ang)[:, :, None, :]
    sin = jnp.sin(ang)[:, :, None, :]
    x1 = x[..., :half].astype(jnp.float32)
    x2 = x[..., half:].astype(jnp.float32)
    out = jnp.concatenate([x1 * cos - x2 * sin, x2 * cos + x1 * sin], axis=-1)
    return out.astype(x.dtype)


def retention(q, k, v, positions):
    B, S, H, D = q.shape
    C = RET_CHUNK
    NC = S // C
    q = rotary(q, positions)
    k = rotary(k, positions) * (D ** -0.5)
    log_g = jnp.log1p(-(2.0 ** (-5.0 - jnp.arange(H, dtype=jnp.float32))))
    idx = jnp.arange(C, dtype=jnp.float32)
    rel = idx[:, None] - idx[None, :]
    decay_intra = jnp.where(rel >= 0, jnp.exp(log_g[:, None, None] * jnp.maximum(rel, 0.0)), 0.0)
    xi = jnp.exp(log_g[:, None] * (idx + 1.0))
    zeta = jnp.exp(log_g[:, None] * (C - 1.0 - idx))
    chunk_decay = jnp.exp(log_g * C)
    qc = q.reshape(B, NC, C, H, D)
    kc = k.reshape(B, NC, C, H, D)
    vc = v.reshape(B, NC, C, H, D)
    scores = jnp.einsum('bnihd,bnjhd->bnhij', qc, kc) * decay_intra
    y_intra = jnp.einsum('bnhij,bnjhd->bnihd', scores, vc)
    kv = jnp.einsum('bnjhd,hj,bnjhe->nbhde', kc, zeta, vc).astype(jnp.float32)

    def step(state, kv_i):
        return chunk_decay[None, :, None, None] * state + kv_i, state

    _, states = lax.scan(step, jnp.zeros((B, H, D, D), jnp.float32), kv)
    y_cross = jnp.einsum('bnihd,nbhde,hi->bnihe', qc, states, xi)
    return (y_intra + y_cross).reshape(B, S, H, D)


def head_group_norm(y, gain):
    B, S, H, D = y.shape
    yf = y.astype(jnp.float32)
    mu = jnp.mean(yf, axis=-1, keepdims=True)
    var = jnp.mean(jnp.square(yf - mu), axis=-1, keepdims=True)
    yn = (yf - mu) * lax.rsqrt(var + EPS)
    return yn.reshape(B, S, H * D) * gain.astype(jnp.float32)


def stick_breaking(q, k, v):
    B, S, H, D = q.shape
    NB = S // SB_BLOCK
    q_blocks = q.reshape(B, NB, SB_BLOCK, H, D).transpose(1, 0, 3, 2, 4)
    kt = k.transpose(0, 2, 1, 3)
    vt = v.transpose(0, 2, 1, 3)
    key_pos = jnp.arange(S)
    scale = D ** -0.5

    def block(args):
        qi, start = args
        z = jnp.einsum('bhqd,bhsd->bhqs', qi, kt).astype(jnp.float32) * scale
        q_pos = start + jnp.arange(SB_BLOCK)
        mask = key_pos[None, :] < q_pos[:, None]
        log_beta = jax.nn.log_sigmoid(z)
        log_stay = jnp.where(mask, log_beta - z, 0.0)
        log_w = log_beta + lax.cumsum(log_stay, axis=3, reverse=True) - log_stay
        w = jnp.where(mask, jnp.exp(log_w), 0.0)
        return jnp.einsum('bhqs,bhsd->bhqd', w.astype(vt.dtype), vt)

    starts = jnp.arange(NB) * SB_BLOCK
    out = lax.map(block, (q_blocks, starts))
    return out.transpose(1, 0, 3, 2, 4).reshape(B, S, H, D)


def peer(h, w_query, sub_keys, expert_down, expert_up):
    B, S, Dm = h.shape
    T = B * S
    K = PEER_TOPK
    ht = h.reshape(T, Dm)
    qry = (ht @ w_query).reshape(T, PEER_HEADS, 2, PEER_HALF)
    sub_scores = jnp.einsum('thpc,hpkc->thpk', qry, sub_keys).astype(jnp.float32)
    top_s, top_i = lax.top_k(sub_scores, K)
    cand_s = top_s[:, :, 0, :, None] + top_s[:, :, 1, None, :]
    cand_id = top_i[:, :, 0, :, None] * N_KEYS + top_i[:, :, 1, None, :]
    best_s, best_c = lax.top_k(cand_s.reshape(T, PEER_HEADS, K * K), K)
    expert_id = jnp.take_along_axis(cand_id.reshape(T, PEER_HEADS, K * K), best_c, axis=-1)
    gate = jax.nn.softmax(best_s, axis=-1)
    NBK = T // PEER_BLOCK
    ids = expert_id.reshape(NBK, PEER_BLOCK, PEER_HEADS * K)
    gates = gate.reshape(NBK, PEER_BLOCK, PEER_HEADS * K)
    xs = ht.reshape(NBK, PEER_BLOCK, Dm)

    def block(args):
        xb, ib, gb = args
        u = expert_down[ib]
        a = jnp.einsum('pd,ped->pe', xb, u).astype(jnp.float32)
        act = jax.nn.gelu(a, approximate=False) * gb
        v = expert_up[ib]
        return jnp.einsum('pe,ped->pd', act.astype(xb.dtype), v)

    y = lax.map(block, (xs, ids, gates))
    return y.reshape(B, S, Dm)


def setup_inputs(seed: int = 0) -> dict:
    key = jax.random.key(seed)
    ks = jax.random.split(key, 16)
    f32 = jnp.float32
    nrm = lambda k, shape, s: jax.random.normal(k, shape, f32) * s
    return {
        "x": nrm(ks[0], (BATCH, SEQ, D_MODEL), 1.0),
        "c": nrm(ks[1], (BATCH, D_MODEL), 1.0),
        "positions": jnp.broadcast_to(jnp.arange(SEQ, dtype=jnp.int32)[None, :], (BATCH, SEQ)),
        "ada_w": nrm(ks[2], (DEPTH, D_MODEL, 6 * D_MODEL), D_MODEL ** -0.5),
        "ada_b": nrm(ks[3], (DEPTH, 6 * D_MODEL), 0.1),
        "norm1_gain": 1.0 + nrm(ks[4], (DEPTH, D_MODEL), 0.1),
        "norm2_gain": 1.0 + nrm(ks[5], (DEPTH, D_MODEL), 0.1),
        "w_in": nrm(ks[6], (DEPTH, D_MODEL, IN_COLS), D_MODEL ** -0.5),
        "ret_norm_gain": 1.0 + nrm(ks[7], (DEPTH, RET_WIDTH), 0.1),
        "sb_q_gain": 1.0 + nrm(ks[8], (DEPTH, HEAD_DIM), 0.1),
        "sb_k_gain": 1.0 + nrm(ks[9], (DEPTH, HEAD_DIM), 0.1),
        "sb_out_gain": 1.0 + nrm(ks[10], (DEPTH, HEAD_DIM), 0.1),
        "w_out": nrm(ks[11], (DEPTH, D_MIX, D_MODEL), D_MIX ** -0.5),
        "peer_w_query": nrm(ks[12], (DEPTH, D_MODEL, PEER_HEADS * PEER_QDIM), D_MODEL ** -0.5),
        "peer_sub_keys": nrm(ks[13], (DEPTH, PEER_HEADS, 2, N_KEYS, PEER_HALF), PEER_HALF ** -0.5),
        "peer_down": nrm(ks[14], (DEPTH, N_EXPERTS, D_MODEL), D_MODEL ** -0.5),
        "peer_up": nrm(ks[15], (DEPTH, N_EXPERTS, D_MODEL), PEER_HEADS ** -0.5),
    }


def reference(x, c, positions, ada_w, ada_b, norm1_gain, norm2_gain, w_in, ret_norm_gain,
              sb_q_gain, sb_k_gain, sb_out_gain, w_out, peer_w_query, peer_sub_keys,
              peer_down, peer_up):
    B, S, _ = x.shape
    for layer in range(DEPTH):
        mod = jax.nn.silu(c) @ ada_w[layer] + ada_b[layer]
        shift1, scale1, gate1, shift2, scale2, gate2 = jnp.split(mod[:, None, :], 6, axis=-1)

        h = rms_norm(x, norm1_gain[layer]) * (1.0 + scale1) + shift1
        proj = h @ w_in[layer]
        r_q, r_k, r_v, r_g, s_q, s_k, s_v = jnp.split(
            proj, np.cumsum([RET_WIDTH] * 4 + [SB_WIDTH] * 2), axis=-1)
        heads_r = lambda t: t.reshape(B, S, N_RET_HEADS, HEAD_DIM)
        heads_s = lambda t: t.reshape(B, S, N_SB_HEADS, HEAD_DIM)

        y_ret = retention(heads_r(r_q), heads_r(r_k), heads_r(r_v), positions)
        y_ret = head_group_norm(y_ret, ret_norm_gain[layer]) * jax.nn.silu(r_g.astype(jnp.float32))

        q_sb = rms_norm(heads_s(s_q), sb_q_gain[layer])
        k_sb = rms_norm(heads_s(s_k), sb_k_gain[layer])
        y_sb = stick_breaking(q_sb, k_sb, heads_s(s_v))
        y_sb = rms_norm(y_sb, sb_out_gain[layer]).reshape(B, S, SB_WIDTH)

        mixed = jnp.concatenate([y_ret.astype(x.dtype), y_sb.astype(x.dtype)], axis=-1) @ w_out[layer]
        x = x + gate1 * mixed

        h2 = rms_norm(x, norm2_gain[layer]) * (1.0 + scale2) + shift2
        ffn = peer(h2, peer_w_query[layer], peer_sub_keys[layer], peer_down[layer], peer_up[layer])
        x = x + gate2 * ffn
    return x
```

```python
import functools

import numpy as np
import jax
import jax.numpy as jnp
from jax import lax
from jax.experimental import pallas as pl
from jax.experimental.pallas import tpu as pltpu

D_MODEL = 1024
HEAD_DIM = 64
HALF_DIM = HEAD_DIM // 2
N_HEADS = 8
GROUP_WIDTH = N_HEADS * HEAD_DIM
CHUNK = 128
ROPE_BASE = 10000.0
N_KEYS = 128
N_EXPERTS = N_KEYS * N_KEYS
PEER_HEADS = 8
PEER_TOPK = 16
PEER_HALF = 128
EPS = 1e-6

LANES = 128
VMEM_LIMIT = 56 * 1024 * 1024

SB_EXIT = -104.0

F32 = jnp.float32
BF16 = jnp.bfloat16
NT_DIMS = (((1,), (1,)), ((), ()))
TN_DIMS = (((0,), (0,)), ((), ()))


def _params(*semantics):
    return pltpu.CompilerParams(dimension_semantics=semantics, vmem_limit_bytes=VMEM_LIMIT)


def _full(shape):
    n = len(shape)
    return pl.BlockSpec(shape, lambda *_: (0,) * n)


def _adaln_kernel(c_ref, w_ref, b_ref, o_ref):
    c = c_ref[...]
    s = c * jax.nn.sigmoid(c)
    o_ref[...] = jnp.dot(s, w_ref[...], preferred_element_type=F32,
                         precision=lax.Precision.HIGHEST) + b_ref[...]


def _adaln(c, ada_w, ada_b):
    n_out = ada_w.shape[1]
    c8 = jnp.broadcast_to(c, (8, D_MODEL))
    out = pl.pallas_call(
        _adaln_kernel,
        grid=(n_out // D_MODEL,),
        in_specs=[_full((8, D_MODEL)),
                  pl.BlockSpec((D_MODEL, D_MODEL), lambda j: (0, j)),
                  pl.BlockSpec((1, D_MODEL), lambda j: (0, j))],
        out_specs=pl.BlockSpec((8, D_MODEL), lambda j: (0, j)),
        out_shape=jax.ShapeDtypeStruct((8, n_out), F32),
        compiler_params=_params("arbitrary"),
        name="adaln",
    )(c8, ada_w, ada_b.reshape(1, n_out))
    return out[0:1]


PROJ_TB = 256
(C_RQ, C_RK, C_RV, C_RG, C_SQ, C_SK, C_SV, C_RQROT, C_RKROT) = range(9)
N_COLGROUPS = 9


def _cols(proj, g):
    return proj[:, g * GROUP_WIDTH:(g + 1) * GROUP_WIDTH]


def _head(a, h):
    return a[:, h * HEAD_DIM:(h + 1) * HEAD_DIM]


def _proj_kernel(x_ref, pos_ref, g1_ref, sc_ref, sh_ref, w_ref, invf_ref, gq_ref, gk_ref,
                 rq_ref, rk_ref, rv_ref, rg_ref, sq_ref, sk_ref, sv_ref):
    x = x_ref[...]
    y = x * lax.rsqrt(jnp.mean(x * x, axis=-1, keepdims=True) + EPS)
    h = (y * g1_ref[...]) * (1.0 + sc_ref[...]) + sh_ref[...]
    proj = jnp.dot(h.astype(BF16), w_ref[...], preferred_element_type=F32)

    ang = pos_ref[...].astype(F32) * invf_ref[...]
    cos = jnp.concatenate([jnp.cos(ang)] * (GROUP_WIDTH // LANES), axis=1)
    sin = jnp.concatenate([jnp.sin(ang)] * (GROUP_WIDTH // LANES), axis=1)
    rq = _cols(proj, C_RQ) * cos + _cols(proj, C_RQROT) * sin
    rk = (_cols(proj, C_RK) * cos + _cols(proj, C_RKROT) * sin) * (HEAD_DIM ** -0.5)
    rv = _cols(proj, C_RV)
    rg_ref[...] = _cols(proj, C_RG)
    sq = _cols(proj, C_SQ)
    sk = _cols(proj, C_SK)
    sv = _cols(proj, C_SV)
    gq = gq_ref[...]
    gk = gk_ref[...]
    for hd in range(N_HEADS):
        rq_ref[hd] = _head(rq, hd)
        rk_ref[hd] = _head(rk, hd)
        rv_ref[hd] = _head(rv, hd)
        q = _head(sq, hd)
        q = q * lax.rsqrt(jnp.mean(q * q, axis=-1, keepdims=True) + EPS) * gq
        sq_ref[hd] = (q * (HEAD_DIM ** -0.5)).astype(BF16)
        k = _head(sk, hd)
        k = k * lax.rsqrt(jnp.mean(k * k, axis=-1, keepdims=True) + EPS) * gk
        sk_ref[hd] = k.astype(BF16)
        sv_ref[hd] = _head(sv, hd).astype(BF16)


def _rotate_half_cols(w):
    d_in = w.shape[0]
    w4 = w.reshape(d_in, N_HEADS, 2, HALF_DIM)
    return jnp.concatenate([-w4[:, :, 1:], w4[:, :, :1]], axis=2).reshape(d_in, GROUP_WIDTH)


def _proj(x2, pos2, g1, scale1, shift1, w_in, gq, gk):
    t = x2.shape[0]
    tb = PROJ_TB
    w_ext = jnp.concatenate(
        [w_in, _rotate_half_cols(_cols(w_in, C_RQ)), _rotate_half_cols(_cols(w_in, C_RK))],
        axis=1).astype(BF16)
    inv_freq = ROPE_BASE ** (-jnp.arange(HALF_DIM, dtype=F32) / HALF_DIM)
    invf = jnp.tile(inv_freq, LANES // HALF_DIM).reshape(1, LANES)
    heads_f32 = jax.ShapeDtypeStruct((N_HEADS, t, HEAD_DIM), F32)
    heads_bf16 = jax.ShapeDtypeStruct((N_HEADS, t, HEAD_DIM), BF16)
    head_spec = pl.BlockSpec((N_HEADS, tb, HEAD_DIM), lambda i: (0, i, 0))
    row = _full((1, D_MODEL))
    return pl.pallas_call(
        _proj_kernel,
        grid=(t // tb,),
        in_specs=[pl.BlockSpec((tb, D_MODEL), lambda i: (i, 0)),
                  pl.BlockSpec((tb, 1), lambda i: (i, 0)),
                  row, row, row,
                  _full((D_MODEL, N_COLGROUPS * GROUP_WIDTH)),
                  _full((1, LANES)), _full((1, HEAD_DIM)), _full((1, HEAD_DIM))],
        out_specs=[head_spec, head_spec, head_spec,
                   pl.BlockSpec((tb, GROUP_WIDTH), lambda i: (i, 0)),
                   head_spec, head_spec, head_spec],
        out_shape=[heads_f32, heads_f32, heads_f32,
                   jax.ShapeDtypeStruct((t, GROUP_WIDTH), F32),
                   heads_bf16, heads_bf16, heads_bf16],
        compiler_params=_params("arbitrary"),
        name="proj",
    )(x2, pos2, g1, scale1, shift1, w_ext, invf, gq, gk)


def _ret_kernel(q_ref, k_ref, v_ref, g_ref, gain_ref, decay_ref, xi_ref, zeta_ref, cd_ref,
                o_ref, state_ref):
    @pl.when(pl.program_id(0) == 0)
    def _():
        state_ref[...] = jnp.zeros_like(state_ref)

    g = g_ref[...]
    gain = gain_ref[...]
    outs = []
    for hd in range(N_HEADS):
        q = q_ref[hd]
        k = k_ref[hd]
        vb = v_ref[hd].astype(BF16)
        qb = q.astype(BF16)
        state = state_ref[hd]
        scores = lax.dot_general(qb, k.astype(BF16), NT_DIMS, preferred_element_type=F32) * decay_ref[hd]
        y = jnp.dot(scores.astype(BF16), vb, preferred_element_type=F32)
        y = y + jnp.dot(qb, state.astype(BF16), preferred_element_type=F32) * xi_ref[hd]
        kv = lax.dot_general((k * zeta_ref[hd]).astype(BF16), vb, TN_DIMS, preferred_element_type=F32)
        state_ref[hd] = cd_ref[hd] * state + kv
        mu = jnp.mean(y, axis=-1, keepdims=True)
        yc = y - mu
        var = jnp.mean(yc * yc, axis=-1, keepdims=True)
        yn = yc * lax.rsqrt(var + EPS)
        gh = _head(g, hd)
        outs.append(yn * _head(gain, hd) * (gh * jax.nn.sigmoid(gh)))
    o_ref[...] = jnp.concatenate(outs, axis=1).astype(o_ref.dtype)


def _retention(rq, rk, rv, rg, gain):
    t = rq.shape[1]
    c = CHUNK
    log_g = jnp.log1p(-(2.0 ** (-5.0 - jnp.arange(N_HEADS, dtype=F32))))
    idx = jnp.arange(c, dtype=F32)
    rel = idx[:, None] - idx[None, :]
    decay = jnp.where(rel >= 0, jnp.exp(log_g[:, None, None] * jnp.maximum(rel, 0.0)), 0.0)
    xi = jnp.exp(log_g[:, None] * (idx + 1.0))
    zeta = jnp.exp(log_g[:, None] * (c - 1.0 - idx))
    chunk_decay = jnp.exp(log_g * c)
    xi_b = jnp.broadcast_to(xi[:, :, None], (N_HEADS, c, HEAD_DIM))
    zeta_b = jnp.broadcast_to(zeta[:, :, None], (N_HEADS, c, HEAD_DIM))
    cd_b = jnp.broadcast_to(chunk_decay[:, None, None], (N_HEADS, HEAD_DIM, HEAD_DIM))
    head_spec = pl.BlockSpec((N_HEADS, c, HEAD_DIM), lambda i: (0, i, 0))
    return pl.pallas_call(
        _ret_kernel,
        grid=(t // c,),
        in_specs=[head_spec, head_spec, head_spec,
                  pl.BlockSpec((c, GROUP_WIDTH), lambda i: (i, 0)),
                  _full((1, GROUP_WIDTH)),
                  _full((N_HEADS, c, c)), _full((N_HEADS, c, HEAD_DIM)),
                  _full((N_HEADS, c, HEAD_DIM)), _full((N_HEADS, HEAD_DIM, HEAD_DIM))],
        out_specs=pl.BlockSpec((c, GROUP_WIDTH), lambda i: (i, 0)),
        out_shape=jax.ShapeDtypeStruct((t, GROUP_WIDTH), BF16),
        scratch_shapes=[pltpu.VMEM((N_HEADS, HEAD_DIM, HEAD_DIM), F32)],
        compiler_params=_params("arbitrary"),
        name="ret",
    )(rq, rk, rv, rg, gain, decay, xi_b, zeta_b, cd_b)


SB_PAIR = 2


def _sb_kernel(q_ref, k_ref, v_ref, gout_ref, tri_ref, o_ref, carry_ref, acc_ref):
    i = pl.program_id(1)
    carry_ref[...] = jnp.zeros_like(carry_ref)
    acc_ref[...] = jnp.zeros_like(acc_ref)
    row = lax.broadcasted_iota(jnp.int32, (CHUNK, CHUNK), 0)
    col = lax.broadcasted_iota(jnp.int32, (CHUNK, CHUNK), 1)
    strict = col < row

    def cond(c):
        j, done = c
        return jnp.logical_and(j >= 0, done == 0)

    def body(c):
        j, _ = c
        off = pl.multiple_of(j * CHUNK, CHUNK)
        mask = jnp.logical_or(strict, j < i)
        tri = tri_ref[...]
        cmax = None
        for hh in range(SB_PAIR):
            kb = k_ref[hh, pl.ds(off, CHUNK), :]
            vb = v_ref[hh, pl.ds(off, CHUNK), :]
            z = lax.dot_general(q_ref[hh], kb, NT_DIMS, preferred_element_type=F32)
            log_beta = jnp.minimum(z, 0.0) - jnp.log1p(jnp.exp(-jnp.abs(z)))
            log_stay = jnp.where(mask, log_beta - z, 0.0)
            hi = log_stay.astype(BF16)
            lo = (log_stay - hi.astype(F32)).astype(BF16)
            sums = (jnp.dot(hi, tri, preferred_element_type=F32)
                    + jnp.dot(lo, tri, preferred_element_type=F32))
            carry = carry_ref[hh]
            log_w = log_beta + sums[:, :CHUNK] + carry
            w = jnp.where(mask, jnp.exp(log_w), 0.0)
            acc_ref[hh] += jnp.dot(w.astype(BF16), vb, preferred_element_type=F32)
            carry = carry + sums[:, CHUNK:]
            carry_ref[hh] = carry
            m = jnp.max(carry)
            cmax = m if cmax is None else jnp.maximum(cmax, m)
        done = (cmax < SB_EXIT).astype(jnp.int32)
        return j - 1, done

    lax.while_loop(cond, body, (i, jnp.int32(0)))

    gout = gout_ref[...]
    outs = []
    for hh in range(SB_PAIR):
        y = acc_ref[hh]
        outs.append(y * lax.rsqrt(jnp.mean(y * y, axis=-1, keepdims=True) + EPS) * gout)
    o_ref[...] = jnp.concatenate(outs, axis=1).astype(o_ref.dtype)


def _stick_breaking(sq, sk, sv, gout):
    t = sq.shape[1]
    c = CHUNK
    r = jnp.arange(c)
    tri = jnp.concatenate([(r[:, None] > r[None, :]).astype(BF16), jnp.ones((c, c), BF16)], axis=1)
    kv_spec = pl.BlockSpec((SB_PAIR, t, HEAD_DIM), lambda p, i: (p, 0, 0))
    return pl.pallas_call(
        _sb_kernel,
        grid=(N_HEADS // SB_PAIR, t // c),
        in_specs=[pl.BlockSpec((SB_PAIR, c, HEAD_DIM), lambda p, i: (p, i, 0)),
                  kv_spec, kv_spec,
                  _full((1, HEAD_DIM)), _full((c, 2 * c))],
        out_specs=pl.BlockSpec((c, SB_PAIR * HEAD_DIM), lambda p, i: (i, p)),
        out_shape=jax.ShapeDtypeStruct((t, GROUP_WIDTH), BF16),
        scratch_shapes=[pltpu.VMEM((SB_PAIR, c, c), F32), pltpu.VMEM((SB_PAIR, c, HEAD_DIM), F32)],
        compiler_params=_params("arbitrary", "arbitrary"),
        name="sb",
    )(sq, sk, sv, gout, tri)


MIX_TB = 256


def _mix_kernel(yr_ref, ys_ref, x_ref, w_ref, gate_ref, g2_ref, sc_ref, sh_ref, x1_ref, h2t_ref):
    mixed = (jnp.dot(yr_ref[...], w_ref[:GROUP_WIDTH, :], preferred_element_type=F32)
             + jnp.dot(ys_ref[...], w_ref[GROUP_WIDTH:, :], preferred_element_type=F32))
    x1 = x_ref[...] + gate_ref[...] * mixed
    x1_ref[...] = x1
    y = x1 * lax.rsqrt(jnp.mean(x1 * x1, axis=-1, keepdims=True) + EPS)
    h2 = (y * g2_ref[...]) * (1.0 + sc_ref[...]) + sh_ref[...]
    h2t_ref[...] = h2.T.astype(BF16)


def _mix(y_ret, y_sb, x2, w_out, gate1, g2, scale2, shift2):
    t = x2.shape[0]
    tb = MIX_TB
    row = _full((1, D_MODEL))
    return pl.pallas_call(
        _mix_kernel,
        grid=(t // tb,),
        in_specs=[pl.BlockSpec((tb, GROUP_WIDTH), lambda i: (i, 0)),
                  pl.BlockSpec((tb, GROUP_WIDTH), lambda i: (i, 0)),
                  pl.BlockSpec((tb, D_MODEL), lambda i: (i, 0)),
                  _full((D_MODEL, D_MODEL)), row, row, row, row],
        out_specs=[pl.BlockSpec((tb, D_MODEL), lambda i: (i, 0)),
                   pl.BlockSpec((D_MODEL, tb), lambda i: (0, i))],
        out_shape=[jax.ShapeDtypeStruct((t, D_MODEL), F32),
                   jax.ShapeDtypeStruct((D_MODEL, t), BF16)],
        compiler_params=_params("arbitrary"),
        name="mix",
    )(y_ret, y_sb, x2, w_out.astype(BF16), gate1, g2, scale2, shift2)


ROUTE_TB = LANES
NEG_INF = float("-inf")
POS_INF = float("inf")


def _top_values(vals, k):
    tops = []
    for r in range(k):
        m = jnp.max(vals, axis=0, keepdims=True)
        tops.append(m)
        if r + 1 < k:
            vals = jnp.where(vals == m, NEG_INF, vals)
    return tops


def _route_kernel(h2t_ref, wq_ref, keys_ref, thr_ref, f1_ref, s2_ref, e2_ref, qry_ref):
    qry = jnp.dot(wq_ref[...], h2t_ref[...], preferred_element_type=F32)
    qry_ref[...] = qry.astype(BF16).reshape(2 * PEER_HEADS, PEER_HALF, ROUTE_TB)

    def per_head(hd, _):
        s1 = jnp.dot(keys_ref[hd, 0], qry_ref[2 * hd], preferred_element_type=F32)
        s2 = jnp.dot(keys_ref[hd, 1], qry_ref[2 * hd + 1], preferred_element_type=F32)
        top1 = _top_values(s1, PEER_TOPK)
        top2 = _top_values(s2, PEER_TOPK)
        top2_col = jnp.concatenate(top2, axis=0)
        cands = [top1[a] + top2_col for a in range(PEER_TOPK)]
        best = _top_values(jnp.concatenate(cands, axis=0), PEER_TOPK)
        tau = best[-1]
        m0 = best[0]
        z = jnp.exp(best[0] - m0)
        for r in range(1, PEER_TOPK):
            z = z + jnp.exp(best[r] - m0)
        thr = jnp.full((N_KEYS, ROUTE_TB), POS_INF, F32)
        for a in range(PEER_TOPK):
            theta = jnp.min(jnp.where(cands[a] >= tau, top2_col, POS_INF), axis=0, keepdims=True)
            thr = jnp.where(s1 == top1[a], theta, thr)
        thr_ref[hd] = thr
        f1_ref[hd] = jnp.exp(s1 - top1[0]) * (jnp.exp(top1[0] + top2[0] - m0) / z)
        s2_ref[hd] = jnp.where(s2 >= top2[-1], s2, NEG_INF)
        e2_ref[hd] = jnp.exp(s2 - top2[0])
        return 0

    lax.fori_loop(0, PEER_HEADS, per_head, 0)


def _route(h2t, w_query, sub_keys):
    t = h2t.shape[1]
    tb = ROUTE_TB
    wq_t = w_query.T.astype(BF16)
    table = jax.ShapeDtypeStruct((PEER_HEADS, N_KEYS, t), F32)
    table_spec = pl.BlockSpec((PEER_HEADS, N_KEYS, tb), lambda i: (0, 0, i))
    return pl.pallas_call(
        _route_kernel,
        grid=(t // tb,),
        in_specs=[pl.BlockSpec((D_MODEL, tb), lambda i: (0, i)),
                  _full(wq_t.shape),
                  _full((PEER_HEADS, 2, N_KEYS, PEER_HALF))],
        out_specs=[table_spec] * 4,
        out_shape=[table] * 4,
        scratch_shapes=[pltpu.VMEM((2 * PEER_HEADS, PEER_HALF, tb), BF16)],
        compiler_params=_params("arbitrary"),
        name="route",
    )(h2t, wq_t, sub_keys.astype(BF16))


PEER_TB = 512
SUBLANES = 8
PEER_EB = SUBLANES * N_KEYS
N_GROUPS = N_EXPERTS // PEER_EB
SQRT_HALF = float(np.sqrt(0.5))


def _peer_kernel(h2t_ref, down_ref, upt_ref, thr_ref, f1_ref, s2_ref, e2_ref, x1_ref, gate_ref,
                 o_ref, acc_ref, a_ref, act_ref):
    e = pl.program_id(1)

    @pl.when(e == 0)
    def _():
        acc_ref[...] = jnp.zeros_like(acc_ref)

    a_ref[...] = jnp.dot(down_ref[...], h2t_ref[...], preferred_element_type=F32)
    for b in range(SUBLANES):
        rows = slice(b * N_KEYS, (b + 1) * N_KEYS)
        for tt in range(PEER_TB // LANES):
            lanes = slice(tt * LANES, (tt + 1) * LANES)
            w = jnp.zeros((N_KEYS, LANES), F32)
            for hd in range(PEER_HEADS):
                picked = s2_ref[hd, :, lanes] >= thr_ref[hd, e, b:b + 1, lanes]
                w = w + jnp.where(picked, e2_ref[hd, :, lanes] * f1_ref[hd, e, b:b + 1, lanes], 0.0)
            a = a_ref[rows, lanes]
            gelu = 0.5 * a * (1.0 + lax.erf(a * SQRT_HALF))
            act_ref[rows, lanes] = (gelu * w).astype(BF16)
    acc_ref[...] += jnp.dot(upt_ref[...], act_ref[...], preferred_element_type=F32)

    @pl.when(e == pl.num_programs(1) - 1)
    def _():
        o_ref[...] = x1_ref[...] + gate_ref[...] * acc_ref[...].T


def _peer(h2t, x1, gate2, tables, peer_down, peer_up):
    t = x1.shape[0]
    tb, eb = PEER_TB, PEER_EB
    down = peer_down.astype(BF16)
    up_t = peer_up.T.astype(BF16)
    table_spec = pl.BlockSpec((PEER_HEADS, N_KEYS, tb), lambda i, e: (0, 0, i))
    grouped_spec = pl.BlockSpec((PEER_HEADS, N_GROUPS, SUBLANES, tb), lambda i, e: (0, 0, 0, i))
    thr, f1, s2, e2 = tables
    thr = thr.reshape(PEER_HEADS, N_GROUPS, SUBLANES, t)
    f1 = f1.reshape(PEER_HEADS, N_GROUPS, SUBLANES, t)
    return pl.pallas_call(
        _peer_kernel,
        grid=(t // tb, N_EXPERTS // eb),
        in_specs=[pl.BlockSpec((D_MODEL, tb), lambda i, e: (0, i)),
                  pl.BlockSpec((eb, D_MODEL), lambda i, e: (e, 0)),
                  pl.BlockSpec((D_MODEL, eb), lambda i, e: (0, e)),
                  grouped_spec, grouped_spec, table_spec, table_spec,
                  pl.BlockSpec((tb, D_MODEL), lambda i, e: (i, 0)),
                  pl.BlockSpec((1, D_MODEL), lambda i, e: (0, 0))],
        out_specs=pl.BlockSpec((tb, D_MODEL), lambda i, e: (i, 0)),
        out_shape=jax.ShapeDtypeStruct((t, D_MODEL), F32),
        scratch_shapes=[pltpu.VMEM((D_MODEL, tb), F32), pltpu.VMEM((eb, tb), F32),
                        pltpu.VMEM((eb, tb), BF16)],
        compiler_params=_params("arbitrary", "arbitrary"),
        name="peer",
    )(h2t, down, up_t, thr, f1, s2, e2, x1, gate2)


def kernel(x, c, positions, ada_w, ada_b, norm1_gain, norm2_gain, w_in, ret_norm_gain, sb_q_gain,
           sb_k_gain, sb_out_gain, w_out, peer_w_query, peer_sub_keys, peer_down, peer_up):
    batch, seq, _ = x.shape
    assert batch == 1, "one sequence per call"
    depth = ada_w.shape[0]
    x2 = x.reshape(seq, D_MODEL)
    pos2 = positions.reshape(seq, 1)
    for layer in range(depth):
        mod = _adaln(c, ada_w[layer], ada_b[layer])
        shift1, scale1, gate1, shift2, scale2, gate2 = jnp.split(mod, 6, axis=-1)
        rq, rk, rv, rg, sq, sk, sv = _proj(
            x2, pos2, norm1_gain[layer].reshape(1, D_MODEL), scale1, shift1, w_in[layer],
            sb_q_gain[layer].reshape(1, HEAD_DIM), sb_k_gain[layer].reshape(1, HEAD_DIM))
        y_ret = _retention(rq, rk, rv, rg, ret_norm_gain[layer].reshape(1, GROUP_WIDTH))
        y_sb = _stick_breaking(sq, sk, sv, sb_out_gain[layer].reshape(1, HEAD_DIM))
        x1, h2t = _mix(y_ret, y_sb, x2, w_out[layer], gate1,
                       norm2_gain[layer].reshape(1, D_MODEL), scale2, shift2)
        tables = _route(h2t, peer_w_query[layer], peer_sub_keys[layer])
        x2 = _peer(h2t, x1, gate2, tables, peer_down[layer], peer_up[layer])
    return x2.reshape(batch, seq, D_MODEL)
```

```python
import functools

import numpy as np
import jax
import jax.numpy as jnp
from jax import lax
from jax.experimental import pallas as pl
from jax.experimental.pallas import tpu as pltpu

D_MODEL = 1024
HEAD_DIM = 64
HALF_DIM = HEAD_DIM // 2
N_HEADS = 8
GROUP_WIDTH = N_HEADS * HEAD_DIM
CHUNK = 128
ROPE_BASE = 10000.0
N_KEYS = 128
N_EXPERTS = N_KEYS * N_KEYS
PEER_HEADS = 8
PEER_TOPK = 16
PEER_HALF = 128
EPS = 1e-6

LANES = 128
VMEM_LIMIT = 56 * 1024 * 1024

SB_EXIT = -104.0

F32 = jnp.float32
BF16 = jnp.bfloat16
NT_DIMS = (((1,), (1,)), ((), ()))
TN_DIMS = (((0,), (0,)), ((), ()))


def _params(*semantics):
    return pltpu.CompilerParams(dimension_semantics=semantics, vmem_limit_bytes=VMEM_LIMIT)


def _full(shape):
    n = len(shape)
    return pl.BlockSpec(shape, lambda *_: (0,) * n)


def _adaln_kernel(c_ref, w_ref, b_ref, o_ref):
    c = c_ref[...]
    s = c * jax.nn.sigmoid(c)
    o_ref[...] = jnp.dot(s, w_ref[...], preferred_element_type=F32,
                         precision=lax.Precision.HIGHEST) + b_ref[...]


def _adaln(c, ada_w, ada_b):
    n_out = ada_w.shape[1]
    c8 = jnp.broadcast_to(c, (8, D_MODEL))
    out = pl.pallas_call(
        _adaln_kernel,
        grid=(n_out // D_MODEL,),
        in_specs=[_full((8, D_MODEL)),
                  pl.BlockSpec((D_MODEL, D_MODEL), lambda j: (0, j)),
                  pl.BlockSpec((1, D_MODEL), lambda j: (0, j))],
        out_specs=pl.BlockSpec((8, D_MODEL), lambda j: (0, j)),
        out_shape=jax.ShapeDtypeStruct((8, n_out), F32),
        compiler_params=_params("arbitrary"),
        name="adaln",
    )(c8, ada_w, ada_b.reshape(1, n_out))
    return out[0:1]


PROJ_TB = 256
(C_RQ, C_RK, C_RV, C_RG, C_SQ, C_SK, C_SV, C_RQROT, C_RKROT) = range(9)
N_COLGROUPS = 9


def _cols(proj, g):
    return proj[:, g * GROUP_WIDTH:(g + 1) * GROUP_WIDTH]


def _head(a, h):
    return a[:, h * HEAD_DIM:(h + 1) * HEAD_DIM]


def _split3(a):
    hi = a.astype(BF16)
    r = a - hi.astype(F32)
    mid = r.astype(BF16)
    lo = (r - mid.astype(F32)).astype(BF16)
    return hi, mid, lo


def _head_rms_norm(a, ones_bd, gain):
    hi, mid, lo = _split3(a * a)
    ss = (jnp.dot(hi, ones_bd, preferred_element_type=F32)
          + jnp.dot(mid, ones_bd, preferred_element_type=F32)
          + jnp.dot(lo, ones_bd, preferred_element_type=F32))
    return a * lax.rsqrt(ss * (1.0 / HEAD_DIM) + EPS) * gain


def _proj_kernel(x_ref, pos_ref, g1_ref, sc_ref, sh_ref, w_ref, invf_ref, gq_ref, gk_ref, bd_ref,
                 rq_ref, rk_ref, rv_ref, rg_ref, sq_ref, sk_ref, sv_ref):
    x = x_ref[...]
    y = x * lax.rsqrt(jnp.mean(x * x, axis=-1, keepdims=True) + EPS)
    h = (y * g1_ref[...]) * (1.0 + sc_ref[...]) + sh_ref[...]
    proj = jnp.dot(h.astype(BF16), w_ref[...], preferred_element_type=F32)

    ang = pos_ref[...].astype(F32) * invf_ref[...]
    cos = jnp.concatenate([jnp.cos(ang)] * (GROUP_WIDTH // LANES), axis=1)
    sin = jnp.concatenate([jnp.sin(ang)] * (GROUP_WIDTH // LANES), axis=1)
    rq = _cols(proj, C_RQ) * cos + _cols(proj, C_RQROT) * sin
    rk = (_cols(proj, C_RK) * cos + _cols(proj, C_RKROT) * sin) * (HEAD_DIM ** -0.5)
    rv = _cols(proj, C_RV)
    rg_ref[...] = _cols(proj, C_RG)
    ones_bd = bd_ref[...]
    q = _head_rms_norm(_cols(proj, C_SQ), ones_bd, gq_ref[...])
    sq_ref[...] = (q * (HEAD_DIM ** -0.5)).astype(BF16)
    sk_ref[...] = _head_rms_norm(_cols(proj, C_SK), ones_bd, gk_ref[...]).astype(BF16)
    sv_ref[...] = _cols(proj, C_SV).astype(BF16)
    for hd in range(N_HEADS):
        rq_ref[hd] = _head(rq, hd)
        rk_ref[hd] = _head(rk, hd)
        rv_ref[hd] = _head(rv, hd)


def _rotate_half_cols(w):
    d_in = w.shape[0]
    w4 = w.reshape(d_in, N_HEADS, 2, HALF_DIM)
    return jnp.concatenate([-w4[:, :, 1:], w4[:, :, :1]], axis=2).reshape(d_in, GROUP_WIDTH)


def _proj(x2, pos2, g1, scale1, shift1, w_in, gq, gk):
    t = x2.shape[0]
    tb = PROJ_TB
    w_ext = jnp.concatenate(
        [w_in, _rotate_half_cols(_cols(w_in, C_RQ)), _rotate_half_cols(_cols(w_in, C_RK))],
        axis=1).astype(BF16)
    inv_freq = ROPE_BASE ** (-jnp.arange(HALF_DIM, dtype=F32) / HALF_DIM)
    invf = jnp.tile(inv_freq, LANES // HALF_DIM).reshape(1, LANES)
    heads_f32 = jax.ShapeDtypeStruct((N_HEADS, t, HEAD_DIM), F32)
    group_bf16 = jax.ShapeDtypeStruct((t, GROUP_WIDTH), BF16)
    head_spec = pl.BlockSpec((N_HEADS, tb, HEAD_DIM), lambda i: (0, i, 0))
    group_spec = pl.BlockSpec((tb, GROUP_WIDTH), lambda i: (i, 0))
    head_of_lane = jnp.arange(GROUP_WIDTH) // HEAD_DIM
    ones_bd = (head_of_lane[:, None] == head_of_lane[None, :]).astype(BF16)
    gq = jnp.tile(gq, (1, N_HEADS))
    gk = jnp.tile(gk, (1, N_HEADS))
    row = _full((1, D_MODEL))
    return pl.pallas_call(
        _proj_kernel,
        grid=(t // tb,),
        in_specs=[pl.BlockSpec((tb, D_MODEL), lambda i: (i, 0)),
                  pl.BlockSpec((tb, 1), lambda i: (i, 0)),
                  row, row, row,
                  _full((D_MODEL, N_COLGROUPS * GROUP_WIDTH)),
                  _full((1, LANES)), _full((1, GROUP_WIDTH)), _full((1, GROUP_WIDTH)),
                  _full((GROUP_WIDTH, GROUP_WIDTH))],
        out_specs=[head_spec, head_spec, head_spec, group_spec,
                   group_spec, group_spec, group_spec],
        out_shape=[heads_f32, heads_f32, heads_f32,
                   jax.ShapeDtypeStruct((t, GROUP_WIDTH), F32),
                   group_bf16, group_bf16, group_bf16],
        compiler_params=_params("arbitrary"),
        name="proj",
    )(x2, pos2, g1, scale1, shift1, w_ext, invf, gq, gk, ones_bd)


def _ret_kernel(q_ref, k_ref, v_ref, g_ref, gain_ref, decay_ref, xi_ref, zeta_ref, cd_ref,
                o_ref, state_ref):
    @pl.when(pl.program_id(0) == 0)
    def _():
        state_ref[...] = jnp.zeros_like(state_ref)

    g = g_ref[...]
    gain = gain_ref[...]
    outs = []
    for hd in range(N_HEADS):
        q = q_ref[hd]
        k = k_ref[hd]
        vb = v_ref[hd].astype(BF16)
        qb = q.astype(BF16)
        state = state_ref[hd]
        scores = lax.dot_general(qb, k.astype(BF16), NT_DIMS, preferred_element_type=F32) * decay_ref[hd]
        y = jnp.dot(scores.astype(BF16), vb, preferred_element_type=F32)
        y = y + jnp.dot(qb, state.astype(BF16), preferred_element_type=F32) * xi_ref[hd]
        kv = lax.dot_general((k * zeta_ref[hd]).astype(BF16), vb, TN_DIMS, preferred_element_type=F32)
        state_ref[hd] = cd_ref[hd] * state + kv
        mu = jnp.mean(y, axis=-1, keepdims=True)
        yc = y - mu
        var = jnp.mean(yc * yc, axis=-1, keepdims=True)
        yn = yc * lax.rsqrt(var + EPS)
        gh = _head(g, hd)
        outs.append(yn * _head(gain, hd) * (gh * jax.nn.sigmoid(gh)))
    o_ref[...] = jnp.concatenate(outs, axis=1).astype(o_ref.dtype)


def _retention(rq, rk, rv, rg, gain):
    t = rq.shape[1]
    c = CHUNK
    log_g = jnp.log1p(-(2.0 ** (-5.0 - jnp.arange(N_HEADS, dtype=F32))))
    idx = jnp.arange(c, dtype=F32)
    rel = idx[:, None] - idx[None, :]
    decay = jnp.where(rel >= 0, jnp.exp(log_g[:, None, None] * jnp.maximum(rel, 0.0)), 0.0)
    xi = jnp.exp(log_g[:, None] * (idx + 1.0))
    zeta = jnp.exp(log_g[:, None] * (c - 1.0 - idx))
    chunk_decay = jnp.exp(log_g * c)
    xi_b = jnp.broadcast_to(xi[:, :, None], (N_HEADS, c, HEAD_DIM))
    zeta_b = jnp.broadcast_to(zeta[:, :, None], (N_HEADS, c, HEAD_DIM))
    cd_b = jnp.broadcast_to(chunk_decay[:, None, None], (N_HEADS, HEAD_DIM, HEAD_DIM))
    head_spec = pl.BlockSpec((N_HEADS, c, HEAD_DIM), lambda i: (0, i, 0))
    return pl.pallas_call(
        _ret_kernel,
        grid=(t // c,),
        in_specs=[head_spec, head_spec, head_spec,
                  pl.BlockSpec((c, GROUP_WIDTH), lambda i: (i, 0)),
                  _full((1, GROUP_WIDTH)),
                  _full((N_HEADS, c, c)), _full((N_HEADS, c, HEAD_DIM)),
                  _full((N_HEADS, c, HEAD_DIM)), _full((N_HEADS, HEAD_DIM, HEAD_DIM))],
        out_specs=pl.BlockSpec((c, GROUP_WIDTH), lambda i: (i, 0)),
        out_shape=jax.ShapeDtypeStruct((t, GROUP_WIDTH), BF16),
        scratch_shapes=[pltpu.VMEM((N_HEADS, HEAD_DIM, HEAD_DIM), F32)],
        compiler_params=_params("arbitrary"),
        name="ret",
    )(rq, rk, rv, rg, gain, decay, xi_b, zeta_b, cd_b)


SB_GROUP = 4
SB_UNROLL = 3
SB_WIDTH = SB_GROUP * HEAD_DIM


def _sb_kernel(q_ref, k_ref, v_ref, gout_ref, tri_ref, o_ref, carry_ref, acc_ref):
    i = pl.program_id(1)
    pairs = SB_GROUP // 2
    row = lax.broadcasted_iota(jnp.int32, (2 * CHUNK, CHUNK), 0)
    col = lax.broadcasted_iota(jnp.int32, (2 * CHUNK, CHUNK), 1)
    strict = col < jnp.where(row < CHUNK, row, row - CHUNK)
    low = lax.broadcasted_iota(jnp.int32, (CHUNK, CHUNK), 1) < HEAD_DIM
    first = jnp.logical_xor(row >= CHUNK, col < HEAD_DIM)
    tri = tri_ref[...]

    qs = []
    for pr in range(pairs):
        qp = q_ref[:, pr * LANES:(pr + 1) * LANES].astype(F32)
        qs.append(jnp.where(first, jnp.concatenate([qp, qp], axis=0), 0.0).astype(BF16))

    def walk(blocks, carries, accs):
        def kv(ref, pr, j):
            return ref[pl.ds(pl.multiple_of(j * CHUNK, CHUNK), CHUNK), pr * LANES:(pr + 1) * LANES]

        zs = [[lax.dot_general(qs[pr], kv(k_ref, pr, j), NT_DIMS, preferred_element_type=F32)
               for j, _ in blocks] for pr in range(pairs)]
        log_betas, parts = [], []
        for pr in range(pairs):
            log_betas.append([])
            for (_, keep), z in zip(blocks, zs[pr]):
                log_beta = jnp.minimum(z, 0.0) - jnp.log1p(jnp.exp(-jnp.abs(z)))
                log_stay = log_beta - z
                if keep is not None:
                    log_stay = jnp.where(keep, log_stay, 0.0)
                hi = log_stay.astype(BF16)
                parts += [hi, (log_stay - hi.astype(F32)).astype(BF16)]
                log_betas[pr].append(log_beta)
        sums = jnp.dot(jnp.concatenate(parts, axis=0), tri, preferred_element_type=F32)
        ws, n = [], 0
        for pr in range(pairs):
            ws.append([])
            for (_, keep), log_beta in zip(blocks, log_betas[pr]):
                s = (sums[n * 2 * CHUNK:(n + 1) * 2 * CHUNK] + sums[(n + 1) * 2 * CHUNK:(n + 2) * 2 * CHUNK])
                n += 2
                w = jnp.exp(log_beta + s[:, :CHUNK] + carries[pr])
                if keep is not None:
                    w = jnp.where(keep, w, 0.0)
                ws[pr].append(w.astype(BF16))
                carries[pr] = carries[pr] + s[:, CHUNK:]
        for pr in range(pairs):
            for (j, _), w in zip(blocks, ws[pr]):
                accs[pr] = accs[pr] + jnp.dot(w, kv(v_ref, pr, j), preferred_element_type=F32)
        cmax = carries[0]
        for pr in range(1, pairs):
            cmax = jnp.maximum(cmax, carries[pr])
        return (jnp.max(cmax) < SB_EXIT).astype(jnp.int32)

    zeros = jnp.zeros((2 * CHUNK, CHUNK), F32)
    carries, accs = [zeros] * pairs, [zeros] * pairs
    done0 = walk([(jnp.maximum(i - d, 0), strict if d == 0 else (i >= d)) for d in range(SB_UNROLL)],
                 carries, accs)
    for pr in range(pairs):
        carry_ref[pr] = carries[pr]
        acc_ref[pr] = accs[pr]

    def cond(c):
        j, done = c
        return jnp.logical_and(j >= 0, done == 0)

    def body(c):
        j, _ = c
        carries = [carry_ref[pr] for pr in range(pairs)]
        accs = [acc_ref[pr] for pr in range(pairs)]
        done = walk([(j, None)], carries, accs)
        for pr in range(pairs):
            carry_ref[pr] = carries[pr]
            acc_ref[pr] = accs[pr]
        return j - 1, done

    lax.while_loop(cond, body, (i - SB_UNROLL, done0))

    gout = gout_ref[...]
    outs = []
    for pr in range(pairs):
        y = jnp.where(low, acc_ref[pr, :CHUNK], acc_ref[pr, CHUNK:])
        yy = y * y
        ms0 = jnp.sum(jnp.where(low, yy, 0.0), axis=-1, keepdims=True) * (1.0 / HEAD_DIM)
        ms1 = jnp.sum(jnp.where(low, 0.0, yy), axis=-1, keepdims=True) * (1.0 / HEAD_DIM)
        outs.append(y * jnp.where(low, lax.rsqrt(ms0 + EPS), lax.rsqrt(ms1 + EPS)) * gout)
    o_ref[...] = jnp.concatenate(outs, axis=1).astype(o_ref.dtype)


def _stick_breaking(sq, sk, sv, gout):
    t = sq.shape[0]
    c = CHUNK
    r = jnp.arange(c)
    tri = jnp.concatenate([(r[:, None] > r[None, :]).astype(BF16), jnp.ones((c, c), BF16)], axis=1)
    kv_spec = pl.BlockSpec((t, SB_WIDTH), lambda g, i: (0, g))
    return pl.pallas_call(
        _sb_kernel,
        grid=(N_HEADS // SB_GROUP, t // c),
        in_specs=[pl.BlockSpec((c, SB_WIDTH), lambda g, i: (i, g)),
                  kv_spec, kv_spec,
                  _full((1, LANES)), _full((c, 2 * c))],
        out_specs=pl.BlockSpec((c, SB_WIDTH), lambda g, i: (i, g)),
        out_shape=jax.ShapeDtypeStruct((t, GROUP_WIDTH), BF16),
        scratch_shapes=[pltpu.VMEM((SB_GROUP // 2, 2 * c, c), F32)] * 2,
        compiler_params=_params("arbitrary", "arbitrary"),
        name="sb",
    )(sq, sk, sv, jnp.tile(gout, (1, LANES // HEAD_DIM)), tri)


MIX_TB = 256


def _mix_kernel(yr_ref, ys_ref, x_ref, w_ref, gate_ref, g2_ref, sc_ref, sh_ref, x1_ref, h2t_ref):
    mixed = (jnp.dot(yr_ref[...], w_ref[:GROUP_WIDTH, :], preferred_element_type=F32)
             + jnp.dot(ys_ref[...], w_ref[GROUP_WIDTH:, :], preferred_element_type=F32))
    x1 = x_ref[...] + gate_ref[...] * mixed
    x1_ref[...] = x1
    y = x1 * lax.rsqrt(jnp.mean(x1 * x1, axis=-1, keepdims=True) + EPS)
    h2 = (y * g2_ref[...]) * (1.0 + sc_ref[...]) + sh_ref[...]
    h2t_ref[...] = h2.T.astype(BF16)


def _mix(y_ret, y_sb, x2, w_out, gate1, g2, scale2, shift2):
    t = x2.shape[0]
    tb = MIX_TB
    row = _full((1, D_MODEL))
    return pl.pallas_call(
        _mix_kernel,
        grid=(t // tb,),
        in_specs=[pl.BlockSpec((tb, GROUP_WIDTH), lambda i: (i, 0)),
                  pl.BlockSpec((tb, GROUP_WIDTH), lambda i: (i, 0)),
                  pl.BlockSpec((tb, D_MODEL), lambda i: (i, 0)),
                  _full((D_MODEL, D_MODEL)), row, row, row, row],
        out_specs=[pl.BlockSpec((tb, D_MODEL), lambda i: (i, 0)),
                   pl.BlockSpec((D_MODEL, tb), lambda i: (0, i))],
        out_shape=[jax.ShapeDtypeStruct((t, D_MODEL), F32),
                   jax.ShapeDtypeStruct((D_MODEL, t), BF16)],
        compiler_params=_params("arbitrary"),
        name="mix",
    )(y_ret, y_sb, x2, w_out.astype(BF16), gate1, g2, scale2, shift2)


ROUTE_TB = 2 * LANES
NEG_INF = float("-inf")
POS_INF = float("inf")


def _top_values(vals, k, with_rank=False):
    tops = []
    rank = jnp.full(vals.shape, float(k), F32) if with_rank else None
    for r in range(k):
        m = jnp.max(vals, axis=0, keepdims=True)
        tops.append(m)
        hit = vals == m
        if with_rank:
            rank = jnp.where(hit, float(r), rank)
        if r + 1 < k:
            vals = jnp.where(hit, NEG_INF, vals)
    return (tops, rank) if with_rank else tops


CAND_WIDTH = (16, 8, 5, 4, 3, 2, 2, 2)
SUBLANES = 8


def _route_kernel(h2t_ref, wq_ref, keys_ref, n1_ref, f1_ref, r2_ref, e2_ref, qry_ref):
    qry = jnp.dot(wq_ref[...], h2t_ref[...], preferred_element_type=F32)
    qry_ref[...] = qry.astype(BF16).reshape(2 * PEER_HEADS, PEER_HALF, ROUTE_TB)
    sub = lax.broadcasted_iota(jnp.int32, (SUBLANES, ROUTE_TB), 0)

    def per_head(hd, _):
        s1 = jnp.dot(keys_ref[hd, 0], qry_ref[2 * hd], preferred_element_type=F32)
        s2 = jnp.dot(keys_ref[hd, 1], qry_ref[2 * hd + 1], preferred_element_type=F32)
        top1 = _top_values(s1, PEER_TOPK)
        top2, rank2 = _top_values(s2, PEER_TOPK, with_rank=True)
        top2_lo = jnp.concatenate(top2[:SUBLANES], axis=0)
        top2_hi = jnp.concatenate(top2[SUBLANES:], axis=0)
        cands = [top1[0] + top2_lo, top1[0] + top2_hi, top1[1] + top2_lo]
        for a in range(2, SUBLANES):
            cands.append(jnp.where(sub < CAND_WIDTH[a], top1[a] + top2_lo, NEG_INF))
        cands.append(jnp.concatenate(top1[SUBLANES:], axis=0) + top2[0])
        best = _top_values(jnp.concatenate(cands, axis=0), PEER_TOPK)
        tau = best[-1]
        m0 = best[0]
        z = jnp.sum(jnp.exp(jnp.concatenate(best, axis=0) - m0), axis=0, keepdims=True)
        counts = [jnp.sum(jnp.where(c >= tau, 1.0, 0.0), axis=0, keepdims=True) for c in cands[:-1]]
        n_rows = [counts[0] + counts[1]] + counts[2:]
        last = jnp.where(cands[-1] >= tau, 1.0, 0.0)
        n_rows += [last[a:a + 1] for a in range(SUBLANES)]
        n1 = jnp.zeros((N_KEYS, ROUTE_TB), F32)
        for a in range(PEER_TOPK):
            n1 = jnp.where(s1 == top1[a], n_rows[a], n1)
        n1_ref[hd] = n1
        f1_ref[hd] = jnp.exp(s1 - top1[0]) * (jnp.exp(top1[0] + top2[0] - m0) / z)
        r2_ref[hd] = pltpu.bitcast(rank2.astype(BF16), jnp.uint32)
        e2_ref[hd] = pltpu.bitcast(jnp.exp(s2 - top2[0]).astype(BF16), jnp.uint32)
        return 0

    lax.fori_loop(0, PEER_HEADS, per_head, 0)


def _route(h2t, w_query, sub_keys):
    t = h2t.shape[1]
    tb = ROUTE_TB
    wq_t = w_query.T.astype(BF16)
    table_spec = pl.BlockSpec((PEER_HEADS, N_KEYS, tb), lambda i: (0, 0, i))
    packed_spec = pl.BlockSpec((PEER_HEADS, N_KEYS // 2, tb), lambda i: (0, 0, i))
    return pl.pallas_call(
        _route_kernel,
        grid=(t // tb,),
        in_specs=[pl.BlockSpec((D_MODEL, tb), lambda i: (0, i)),
                  _full(wq_t.shape),
                  _full((PEER_HEADS, 2, N_KEYS, PEER_HALF))],
        out_specs=[table_spec, table_spec, packed_spec, packed_spec],
        out_shape=[jax.ShapeDtypeStruct((PEER_HEADS, N_KEYS, t), F32)] * 2
        + [jax.ShapeDtypeStruct((PEER_HEADS, N_KEYS // 2, t), jnp.uint32)] * 2,
        scratch_shapes=[pltpu.VMEM((2 * PEER_HEADS, PEER_HALF, tb), BF16)],
        compiler_params=_params("arbitrary"),
        name="route",
    )(h2t, wq_t, sub_keys.astype(BF16))


PEER_TB = 512
PEER_EB = SUBLANES * N_KEYS
N_GROUPS = N_EXPERTS // PEER_EB
SQRT_HALF = float(np.sqrt(0.5))
PACKED_ROWS = 2 * SUBLANES


def _peer_kernel(h2t_ref, down_ref, upt_ref, n1_ref, f1_ref, r2_ref, e2_ref, x1_ref, gate_ref,
                 o_ref, acc_ref, a_ref, act_ref):
    e = pl.program_id(1)

    @pl.when(e == 0)
    def _():
        acc_ref[...] = jnp.zeros_like(acc_ref)

    a_ref[...] = jnp.dot(down_ref[...], h2t_ref[...], preferred_element_type=F32)
    slabs = N_KEYS // PACKED_ROWS
    for b in range(SUBLANES):
        for tt in range(PEER_TB // LANES):
            lanes = slice(tt * LANES, (tt + 1) * LANES)
            w = jnp.zeros((N_KEYS, LANES), BF16)
            for hd in range(PEER_HEADS):
                n_row = jnp.broadcast_to(n1_ref[hd, e, b:b + 1, lanes], (PACKED_ROWS, LANES)).astype(BF16)
                f_row = jnp.broadcast_to(f1_ref[hd, e, b:b + 1, lanes], (PACKED_ROWS, LANES)).astype(BF16)
                n_row = jnp.concatenate([n_row] * slabs, axis=0)
                f_row = jnp.concatenate([f_row] * slabs, axis=0)
                rank2 = pltpu.bitcast(r2_ref[hd, :, lanes], BF16)
                e2 = pltpu.bitcast(e2_ref[hd, :, lanes], BF16)
                w = w + jnp.where(rank2 < n_row, e2 * f_row, 0.0)
            rows = slice(b * N_KEYS, (b + 1) * N_KEYS)
            a = a_ref[rows, lanes]
            gelu = 0.5 * a * (1.0 + lax.erf(a * SQRT_HALF))
            act_ref[rows, lanes] = gelu.astype(BF16) * w
    acc_ref[...] += jnp.dot(upt_ref[...], act_ref[...], preferred_element_type=F32)

    @pl.when(e == pl.num_programs(1) - 1)
    def _():
        o_ref[...] = x1_ref[...] + gate_ref[...] * acc_ref[...].T


def _peer(h2t, x1, gate2, tables, peer_down, peer_up):
    t = x1.shape[0]
    tb, eb = PEER_TB, PEER_EB
    down = peer_down.astype(BF16)
    up_t = peer_up.T.astype(BF16)
    packed_spec = pl.BlockSpec((PEER_HEADS, N_KEYS // 2, tb), lambda i, e: (0, 0, i))
    grouped_spec = pl.BlockSpec((PEER_HEADS, N_GROUPS, SUBLANES, tb), lambda i, e: (0, 0, 0, i))
    n1, f1, r2, e2 = tables
    n1 = n1.reshape(PEER_HEADS, N_GROUPS, SUBLANES, t)
    f1 = f1.reshape(PEER_HEADS, N_GROUPS, SUBLANES, t)
    return pl.pallas_call(
        _peer_kernel,
        grid=(t // tb, N_EXPERTS // eb),
        in_specs=[pl.BlockSpec((D_MODEL, tb), lambda i, e: (0, i)),
                  pl.BlockSpec((eb, D_MODEL), lambda i, e: (e, 0)),
                  pl.BlockSpec((D_MODEL, eb), lambda i, e: (0, e)),
                  grouped_spec, grouped_spec, packed_spec, packed_spec,
                  pl.BlockSpec((tb, D_MODEL), lambda i, e: (i, 0)),
                  pl.BlockSpec((1, D_MODEL), lambda i, e: (0, 0))],
        out_specs=pl.BlockSpec((tb, D_MODEL), lambda i, e: (i, 0)),
        out_shape=jax.ShapeDtypeStruct((t, D_MODEL), F32),
        scratch_shapes=[pltpu.VMEM((D_MODEL, tb), F32), pltpu.VMEM((eb, tb), F32),
                        pltpu.VMEM((eb, tb), BF16)],
        compiler_params=_params("arbitrary", "arbitrary"),
        name="peer",
    )(h2t, down, up_t, n1, f1, r2, e2, x1, gate2)


def kernel(x, c, positions, ada_w, ada_b, norm1_gain, norm2_gain, w_in, ret_norm_gain, sb_q_gain,
           sb_k_gain, sb_out_gain, w_out, peer_w_query, peer_sub_keys, peer_down, peer_up):
    batch, seq, _ = x.shape
    assert batch == 1, "one sequence per call"
    depth = ada_w.shape[0]
    x2 = x.reshape(seq, D_MODEL)
    pos2 = positions.reshape(seq, 1)
    for layer in range(depth):
        mod = _adaln(c, ada_w[layer], ada_b[layer])
        shift1, scale1, gate1, shift2, scale2, gate2 = jnp.split(mod, 6, axis=-1)
        rq, rk, rv, rg, sq, sk, sv = _proj(
            x2, pos2, norm1_gain[layer].reshape(1, D_MODEL), scale1, shift1, w_in[layer],
            sb_q_gain[layer].reshape(1, HEAD_DIM), sb_k_gain[layer].reshape(1, HEAD_DIM))
        y_ret = _retention(rq, rk, rv, rg, ret_norm_gain[layer].reshape(1, GROUP_WIDTH))
        y_sb = _stick_breaking(sq, sk, sv, sb_out_gain[layer].reshape(1, HEAD_DIM))
        x1, h2t = _mix(y_ret, y_sb, x2, w_out[layer], gate1,
                       norm2_gain[layer].reshape(1, D_MODEL), scale2, shift2)
        tables = _route(h2t, peer_w_query[layer], peer_sub_keys[layer])
        x2 = _peer(h2t, x1, gate2, tables, peer_down[layer], peer_up[layer])
    return x2.reshape(batch, seq, D_MODEL)
```
